```python
import math
import jax, jax.numpy as jnp
from jax import lax
import numpy as np

D_MODEL = 1024
BATCH = 4
SEQ = 8192
DEPTH = 4

GRID_W = 64
CTX_LEN = 256
BLOCK_Q = 128
CHUNK = 128
ROPE_BASE = 10000.0
EPS = 1e-6

A_HEADS = 6
A_KV_HEADS = 2
A_GROUP = A_HEADS // A_KV_HEADS
A_HEAD_DIM = 64
A_WIDTH = A_HEADS * A_HEAD_DIM
B_GROUPS = 4
B_GROUP_DIM = 64
B_WIDTH = B_GROUPS * B_GROUP_DIM
C_HEADS = 4
C_QK_DIM = 48
C_V_DIM = 2 * C_QK_DIM
C_WIDTH = C_HEADS * C_V_DIM
MIX_WIDTH = A_WIDTH + B_WIDTH + C_WIDTH
IN_SIZES = (A_WIDTH, A_KV_HEADS * A_HEAD_DIM, A_KV_HEADS * A_HEAD_DIM,
            B_WIDTH, B_WIDTH,
            2 * C_HEADS * C_QK_DIM, 2 * C_HEADS * C_QK_DIM, C_WIDTH)
IN_WIDTH = 2304
D_FF = 2816
N_MOD = 9

kernel_name = "hymba_style_gqa_gmlp_diffattn_macaron_dit"


def rms_norm(x, gain):
    xf = x.astype(jnp.float32)
    y = xf * lax.rsqrt(jnp.mean(xf * xf, axis=-1, keepdims=True) + EPS)
    return (y * gain.astype(jnp.float32)).astype(x.dtype)


def modulate(x, gain, shift, scale):
    return rms_norm(x, gain) * (1 + scale) + shift


def swiglu(h, w_in, w_out):
    gate, up = jnp.split(h @ w_in, 2, axis=-1)
    return (jax.nn.silu(gate) * up) @ w_out


def axial_angles(n, dim):
    rows = n // GRID_W
    row = jnp.repeat(jnp.arange(rows, dtype=jnp.float32), GRID_W)
    col = jnp.tile(jnp.arange(GRID_W, dtype=jnp.float32), rows)
    half = dim // 2
    freqs = 1.0 / (ROPE_BASE ** (jnp.arange(0, half, 2, dtype=jnp.float32) / half))
    return row[:, None] * freqs, col[:, None] * freqs


def rope_rotate(x, ang):
    m = ang.shape[-1]
    shape = (1, ang.shape[0]) + (1,) * (x.ndim - 3) + (m,)
    cos = jnp.cos(ang).reshape(shape).astype(x.dtype)
    sin = jnp.sin(ang).reshape(shape).astype(x.dtype)
    x1, x2 = x[..., :m], x[..., m:]
    return jnp.concatenate([x1 * cos - x2 * sin, x2 * cos + x1 * sin], axis=-1)


def axial_rope(x, ang_row, ang_col):
    half = x.shape[-1] // 2
    return jnp.concatenate([rope_rotate(x[..., :half], ang_row),
                            rope_rotate(x[..., half:], ang_col)], axis=-1)


def sweep_query_blocks(fn, q):
    b, n = q.shape[:2]
    qb = jnp.moveaxis(q.reshape((b, n // BLOCK_Q, BLOCK_Q) + q.shape[2:]), 1, 0)
    ob = jnp.moveaxis(lax.map(fn, qb), 0, 1)
    return ob.reshape((b, n) + ob.shape[3:])


def gqa_attend(q, k, v):
    s = jnp.einsum("bqhgd,bkhd->bhgqk", q, k).astype(jnp.float32) * (A_HEAD_DIM ** -0.5)
    p = jax.nn.softmax(s, axis=-1).astype(v.dtype)
    return jnp.einsum("bhgqk,bkhd->bqhgd", p, v)


def diff_attend(q, k, v, lam):
    s = jnp.einsum("bqchd,bkchd->bchqk", q, k).astype(jnp.float32) * (C_QK_DIM ** -0.5)
    p = jax.nn.softmax(s, axis=-1)
    a = (p[:, 0] - lam * p[:, 1]).astype(v.dtype)
    return jnp.einsum("bhqk,bkhe->bqhe", a, v)


def chunk_spatial_gate(u, v, w_s, b_s):
    b, n, _ = v.shape
    vc = v.reshape(b, n // CHUNK, CHUNK, B_GROUPS, B_GROUP_DIM)
    mixed = jnp.einsum("gpq,bnqgc->bnpgc", w_s, vc) + b_s.T[:, :, None]
    return u * mixed.reshape(b, n, B_WIDTH)


def split_heads(p):
    lead = p.shape[:2]
    idx = [int(i) for i in np.cumsum(IN_SIZES)[:-1]]
    qa, ka, va, u, v, qc, kc, vc = jnp.split(p, idx, axis=-1)
    return (qa.reshape(lead + (A_HEADS, A_HEAD_DIM)),
            ka.reshape(lead + (A_KV_HEADS, A_HEAD_DIM)),
            va.reshape(lead + (A_KV_HEADS, A_HEAD_DIM)),
            u, v,
            qc.reshape(lead + (2, C_HEADS, C_QK_DIM)),
            kc.reshape(lead + (2, C_HEADS, C_QK_DIM)),
            vc.reshape(lead + (C_HEADS, C_V_DIM)))


def token_mixing(hx, hc, w_in, w_out, g_q, g_k, g_v, w_s, b_s, lam_vecs, g_sub,
                 layer, ang_a, ang_c, with_ctx_out):
    b, n, _ = hx.shape
    L = hc.shape[1]
    qa_x, ka_x, va_x, u_x, v_x, qc_x, kc_x, vc_x = split_heads(hx @ w_in)
    qa_c, ka_c, va_c, u_c, v_c, qc_c, kc_c, vc_c = split_heads(hc @ w_in)

    qa_x = axial_rope(rms_norm(qa_x, g_q), *ang_a)
    ka_x = axial_rope(rms_norm(ka_x, g_k), *ang_a)
    qa_c = rms_norm(qa_c, g_q)
    ka_c = rms_norm(ka_c, g_k)
    ka_all = jnp.concatenate([ka_x, ka_c], axis=1)
    va_all = jnp.concatenate([va_x, va_c], axis=1)
    ya_x = sweep_query_blocks(lambda qb: gqa_attend(qb, ka_all, va_all),
                              qa_x.reshape(b, n, A_KV_HEADS, A_GROUP, A_HEAD_DIM)).reshape(b, n, A_WIDTH)

    yb_x = chunk_spatial_gate(jax.nn.gelu(u_x), rms_norm(jax.nn.gelu(v_x), g_v), w_s, b_s)

    lambda_init = 0.8 - 0.6 * math.exp(-0.3 * layer)
    lv = lam_vecs.astype(jnp.float32)
    lam = jnp.exp(jnp.sum(lv[0] * lv[1])) - jnp.exp(jnp.sum(lv[2] * lv[3])) + lambda_init
    qc_x = axial_rope(qc_x, *ang_c)
    kc_x = axial_rope(kc_x, *ang_c)
    kc_all = jnp.concatenate([kc_x, kc_c], axis=1)
    vc_all = jnp.concatenate([vc_x, vc_c], axis=1)
    yc_x = sweep_query_blocks(lambda qb: diff_attend(qb, kc_all, vc_all, lam), qc_x)
    yc_x = (rms_norm(yc_x, g_sub) * (1.0 - lambda_init)).reshape(b, n, C_WIDTH)

    out_x = jnp.concatenate([ya_x, yb_x, yc_x], axis=-1) @ w_out
    if not with_ctx_out:
        return out_x, None

    ya_c = gqa_attend(qa_c.reshape(b, L, A_KV_HEADS, A_GROUP, A_HEAD_DIM), ka_c, va_c).reshape(b, L, A_WIDTH)
    yb_c = chunk_spatial_gate(jax.nn.gelu(u_c), rms_norm(jax.nn.gelu(v_c), g_v), w_s, b_s)
    yc_c = (rms_norm(diff_attend(qc_c, kc_c, vc_c, lam), g_sub) * (1.0 - lambda_init)).reshape(b, L, C_WIDTH)
    out_c = jnp.concatenate([ya_c, yb_c, yc_c], axis=-1) @ w_out
    return out_x, out_c


def setup_inputs(seed: int = 0) -> dict:
    key = jax.random.key(seed)
    ks = jax.random.split(key, 24)
    f32 = jnp.float32
    nrm = lambda k, shape, s: jax.random.normal(k, shape, f32) * s
    gain = lambda k, shape: 1.0 + 0.05 * jax.random.normal(k, shape, f32)
    return {
        "x": nrm(ks[0], (BATCH, SEQ, D_MODEL), 1.0),
        "c": nrm(ks[1], (BATCH, D_MODEL), 1.0),
        "ctx": nrm(ks[2], (BATCH, CTX_LEN, D_MODEL), 1.0),
        "c_ctx": nrm(ks[3], (D_MODEL,), 1.0),
        "w_mod": nrm(ks[4], (DEPTH, D_MODEL, N_MOD * D_MODEL), 0.5 * D_MODEL ** -0.5),
        "b_mod": nrm(ks[5], (DEPTH, N_MOD * D_MODEL), 0.02),
        "g_ffn1": gain(ks[6], (DEPTH, D_MODEL)),
        "w_ffn1_in": nrm(ks[7], (DEPTH, D_MODEL, 2 * D_FF), D_MODEL ** -0.5),
        "w_ffn1_out": nrm(ks[8], (DEPTH, D_FF, D_MODEL), D_FF ** -0.5),
        "g_mix": gain(ks[9], (DEPTH, D_MODEL)),
        "w_mix_in": nrm(ks[10], (DEPTH, D_MODEL, IN_WIDTH), D_MODEL ** -0.5),
        "w_mix_out": nrm(ks[11], (DEPTH, MIX_WIDTH, D_MODEL), MIX_WIDTH ** -0.5),
        "g_qnorm": gain(ks[12], (DEPTH, A_HEAD_DIM)),
        "g_knorm": gain(ks[13], (DEPTH, A_HEAD_DIM)),
        "g_vnorm": gain(ks[14], (DEPTH, B_WIDTH)),
        "w_spatial": nrm(ks[15], (DEPTH, B_GROUPS, CHUNK, CHUNK), CHUNK ** -0.5),
        "b_spatial": gain(ks[16], (DEPTH, B_GROUPS, CHUNK)),
        "lambda_vecs": nrm(ks[17], (DEPTH, 4, C_QK_DIM), 0.1),
        "g_subln": gain(ks[18], (DEPTH, C_V_DIM)),
        "g_ffn2": gain(ks[19], (DEPTH, D_MODEL)),
        "w_ffn2_in": nrm(ks[20], (DEPTH, D_MODEL, 2 * D_FF), D_MODEL ** -0.5),
        "w_ffn2_out": nrm(ks[21], (DEPTH, D_FF, D_MODEL), D_FF ** -0.5),
        "g_final": gain(ks[22], (D_MODEL,)),
    }


def reference(x, c, ctx, c_ctx, w_mod, b_mod, g_ffn1, w_ffn1_in, w_ffn1_out, g_mix, w_mix_in,
              w_mix_out, g_qnorm, g_knorm, g_vnorm, w_spatial, b_spatial, lambda_vecs, g_subln,
              g_ffn2, w_ffn2_in, w_ffn2_out, g_final):
    n = x.shape[1]
    ang_a = axial_angles(n, A_HEAD_DIM)
    ang_c = axial_angles(n, C_QK_DIM)
    sc = jax.nn.silu(c)
    scc = jax.nn.silu(c_ctx)
    for l in range(DEPTH):
        last = l == DEPTH - 1
        mx = jnp.split((sc @ w_mod[l] + b_mod[l])[:, None, :], N_MOD, axis=-1)
        mc = jnp.split(scc @ w_mod[l] + b_mod[l], N_MOD, axis=-1)
        x = x + 0.5 * mx[2] * swiglu(modulate(x, g_ffn1[l], mx[0], mx[1]), w_ffn1_in[l], w_ffn1_out[l])
        ctx = ctx + 0.5 * mc[2] * swiglu(modulate(ctx, g_ffn1[l], mc[0], mc[1]), w_ffn1_in[l], w_ffn1_out[l])
        yx, yc = token_mixing(modulate(x, g_mix[l], mx[3], mx[4]), modulate(ctx, g_mix[l], mc[3], mc[4]),
                              w_mix_in[l], w_mix_out[l], g_qnorm[l], g_knorm[l], g_vnorm[l],
                              w_spatial[l], b_spatial[l], lambda_vecs[l], g_subln[l],
                              l, ang_a, ang_c, not last)
        x = x + mx[5] * yx
        x = x + 0.5 * mx[8] * swiglu(modulate(x, g_ffn2[l], mx[6], mx[7]), w_ffn2_in[l], w_ffn2_out[l])
        if not last:
            ctx = ctx + mc[5] * yc
            ctx = ctx + 0.5 * mc[8] * swiglu(modulate(ctx, g_ffn2[l], mc[6], mc[7]), w_ffn2_in[l], w_ffn2_out[l])
    return rms_norm(x, g_final)
```

```python
import functools
import math

import numpy as np
import jax
import jax.numpy as jnp
from jax import lax
from jax.experimental import pallas as pl
from jax.experimental.pallas import tpu as pltpu

D_MODEL = 1024
D_FF = 2816
N_MOD = 9
GRID_W = 64
CHUNK = 128
ROPE_BASE = 10000.0
EPS = 1e-6
A_HEADS, A_KV_HEADS, A_HEAD_DIM = 6, 2, 64
A_GROUP = A_HEADS // A_KV_HEADS
B_GROUPS, B_GROUP_DIM = 4, 64
B_WIDTH = B_GROUPS * B_GROUP_DIM
C_HEADS, C_QK_DIM = 4, 48
C_V_DIM = 2 * C_QK_DIM

LANES = 128
HALF = LANES // 2
VMEM_LIMIT_BYTES = 56 * 1024 * 1024

TM = 512
TQ = 256
TK = 512
FF_CHUNK = 256
MOD_ROWS = 8

QA_W, KA_W, VA_W, U_W, V_W, QC_W, KC_W, VC_W = 384, 128, 256, 256, 256, 512, 512, 512
_OFFS = np.cumsum([0, QA_W, KA_W, VA_W, U_W, V_W, QC_W, KC_W, VC_W])
QA_O, KA_O, VA_O, U_O, V_O, QC_O, KC_O, VC_O, IN_W = [int(v) for v in _OFFS]
OUT_ROWS = QA_W + B_WIDTH + C_HEADS * LANES

_f32 = jnp.float32
_bf16 = jnp.bfloat16


def _dot(a, b):
    return jnp.dot(a, b, preferred_element_type=_f32)


def _rms(x, gain):
    return x * lax.rsqrt(jnp.mean(x * x, axis=-1, keepdims=True) + EPS) * gain


def _silu(x):
    return x * (1.0 / (1.0 + jnp.exp(-x)))


def _gelu_tanh(x):
    return 0.5 * x * (1.0 + jnp.tanh(math.sqrt(2.0 / math.pi) * (x + 0.044715 * (x * x * x))))


def _lane(shape):
    return lax.broadcasted_iota(jnp.int32, shape, len(shape) - 1)


def _mod_kernel(c_ref, w_ref, b_ref, o_ref):
    sc = _silu(c_ref[...]).astype(_bf16)
    o_ref[...] = _dot(sc, w_ref[...].astype(_bf16)) + b_ref[...]


def _modulation(cc, w_mod, b_mod):
    depth = w_mod.shape[0]
    return pl.pallas_call(
        _mod_kernel,
        grid=(depth, N_MOD),
        in_specs=[
            pl.BlockSpec((MOD_ROWS, D_MODEL), lambda l, j: (0, 0)),
            pl.BlockSpec((None, D_MODEL, D_MODEL), lambda l, j: (l, 0, j)),
            pl.BlockSpec((None, None, 1, D_MODEL), lambda l, j: (l, j, 0, 0)),
        ],
        out_specs=pl.BlockSpec((None, None, MOD_ROWS, D_MODEL), lambda l, j: (l, j, 0, 0)),
        out_shape=jax.ShapeDtypeStruct((depth, N_MOD, MOD_ROWS, D_MODEL), _f32),
        compiler_params=pltpu.CompilerParams(
            dimension_semantics=("arbitrary", "arbitrary"), vmem_limit_bytes=VMEM_LIMIT_BYTES),
        name="modulation",
    )(cc, w_mod, b_mod.reshape(depth, N_MOD, 1, D_MODEL))


def _mod_row(mod_ref, k, row):
    return mod_ref[k, pl.ds(row, 1), :]


def _swiglu_residual(x, gain, shift, scale, gate, w_in_ref, w_out_ref, act_ref):
    h = (_rms(x, gain) * (1.0 + scale) + shift).astype(_bf16)
    for c in range(D_FF // FF_CHUNK):
        lo = c * FF_CHUNK
        g = _dot(h, w_in_ref[:, lo:lo + FF_CHUNK])
        u = _dot(h, w_in_ref[:, D_FF + lo:D_FF + lo + FF_CHUNK])
        act_ref[:, lo:lo + FF_CHUNK] = (_silu(g) * u).astype(_bf16)
    y = _dot(act_ref[...], w_out_ref[...])
    return x + (0.5 * gate) * y


def _head_mean_square(x):
    r = lax.broadcasted_iota(jnp.int32, (LANES, LANES), 0) // HALF
    c = lax.broadcasted_iota(jnp.int32, (LANES, LANES), 1) // HALF
    blockdiag = jnp.where(r == c, 1.0 / HALF, 0.0).astype(_bf16)
    sq = x * x
    hi = sq.astype(_bf16)
    lo = (sq - hi.astype(_f32)).astype(_bf16)
    return _dot(hi, blockdiag) + _dot(lo, blockdiag)


def _rope(x, cos, sin):
    first = (_lane(x.shape) % 32) < 16
    partner = jnp.where(first, pltpu.roll(x, LANES - 16, 1), pltpu.roll(x, 16, 1))
    return x * cos + partner * sin


def _pre_kernel(x_ref, mod_ref, g1_ref, w1i_ref, w1o_ref, gm_ref, wmi_ref, gq_ref, gk_ref, gv_ref,
                ws_ref, bs_ref, tab_ref,
                xo_ref, qa_ref, ka_ref, va_ref, qc_ref, kc_ref, vc_ref, yb_ref,
                act_ref, *, tiles_per_batch, n_latent_tiles, ctx_row):
    i = pl.program_id(0)
    row = jnp.where(i < n_latent_tiles, i // tiles_per_batch, ctx_row)
    m = lambda k: _mod_row(mod_ref, k, row)

    x = _swiglu_residual(x_ref[...], g1_ref[...], m(0), m(1), m(2), w1i_ref, w1o_ref, act_ref)
    xo_ref[...] = x

    h = (_rms(x, gm_ref[...]) * (1.0 + m(4)) + m(3)).astype(_bf16)
    proj = lambda off, width: _dot(h, wmi_ref[:, off:off + width])

    cos_a, sin_a, cos_c, sin_c = tab_ref[0], tab_ref[1], tab_ref[2], tab_ref[3]
    lane = _lane((TM, LANES))

    for c in range(QA_W // LANES):
        q = proj(QA_O + c * LANES, LANES)
        q = q * lax.rsqrt(_head_mean_square(q) + EPS) * gq_ref[...]
        qa_ref[:, c * LANES:(c + 1) * LANES] = (_rope(q, cos_a, sin_a) * (A_HEAD_DIM ** -0.5)).astype(_bf16)
    k = proj(KA_O, LANES)
    k = k * lax.rsqrt(_head_mean_square(k) + EPS) * gk_ref[...]
    ka_ref[...] = _rope(k, cos_a, sin_a).astype(_bf16)
    v0 = proj(VA_O, LANES)
    va_ref[:, :LANES] = jnp.where(lane == HALF, 1.0, v0).astype(_bf16)
    v1 = proj(VA_O + LANES, LANES)
    va_ref[:, LANES:] = jnp.where(lane == 0, 1.0, v1).astype(_bf16)

    for c in range(C_HEADS):
        q = proj(QC_O + c * LANES, LANES)
        qc_ref[:, c * LANES:(c + 1) * LANES] = (_rope(q, cos_c, sin_c) * (C_QK_DIM ** -0.5)).astype(_bf16)
        k = proj(KC_O + c * LANES, LANES)
        kc_ref[:, c * LANES:(c + 1) * LANES] = _rope(k, cos_c, sin_c).astype(_bf16)
        v = proj(VC_O + c * LANES, LANES)
        vc_ref[:, c * LANES:(c + 1) * LANES] = jnp.where(lane == C_V_DIM, 1.0, v).astype(_bf16)

    u = _gelu_tanh(proj(U_O, U_W))
    v = _rms(_gelu_tanh(proj(V_O, V_W)), gv_ref[...]).astype(_bf16)
    lo_half = _lane((CHUNK, LANES)) < HALF
    for t in range(TM // CHUNK):
        rows = slice(t * CHUNK, (t + 1) * CHUNK)
        for c in range(B_WIDTH // LANES):
            cols = slice(c * LANES, (c + 1) * LANES)
            vt = v[rows, cols]
            mixed = jnp.where(lo_half, _dot(ws_ref[2 * c], vt), _dot(ws_ref[2 * c + 1], vt)) + bs_ref[:, cols]
            yb_ref[rows, cols] = (u[rows, cols] * mixed).astype(_bf16)


def _post_kernel(x_ref, mod_ref, ya_ref, yb_ref, yc_ref, wo_ref, g2_ref, w2i_ref, w2o_ref, gf_ref,
                 xo_ref, act_ref, *, tiles_per_batch, n_latent_tiles, ctx_row, final_norm):
    i = pl.program_id(0)
    row = jnp.where(i < n_latent_tiles, i // tiles_per_batch, ctx_row)
    m = lambda k: _mod_row(mod_ref, k, row)
    y = (_dot(ya_ref[...], wo_ref[:QA_W, :]) + _dot(yb_ref[...], wo_ref[QA_W:QA_W + B_WIDTH, :])
         + _dot(yc_ref[...], wo_ref[QA_W + B_WIDTH:, :]))
    x = x_ref[...] + m(5) * y
    x = _swiglu_residual(x, g2_ref[...], m(6), m(7), m(8), w2i_ref, w2o_ref, act_ref)
    if final_norm:
        x = _rms(x, gf_ref[...])
    xo_ref[...] = x


def _attn_kernel(*refs, groups, segments, mode, lambda_init):
    n_seg = len(segments)
    q_ref = refs[0]
    kv_refs = refs[1:1 + 2 * n_seg]
    pos = 1 + 2 * n_seg
    if mode == "diff":
        lv_ref, gs_ref = refs[pos], refs[pos + 1]
        pos += 2
    o_ref = refs[pos]
    m_ref, acc_ref = refs[pos + 1], refs[pos + 2]

    tq = q_ref.shape[0]
    lane = _lane((tq, LANES))
    lo_half = lane < HALF

    for stack, k_col, v_col in groups:
        rows = len(stack) * tq
        qs = jnp.concatenate(
            [jnp.where(lo_half if half == 0 else jnp.logical_not(lo_half),
                       q_ref[:, c * LANES:(c + 1) * LANES], jnp.zeros((), _bf16))
             for c, half in stack], axis=0)
        m_ref[:rows, :] = jnp.full((rows, 1), -1e30, _f32)
        acc_ref[:rows, :] = jnp.zeros((rows, LANES), _f32)

        def step(k, v):
            s = lax.dot_general(qs, k, (((1,), (1,)), ((), ())), preferred_element_type=_f32)
            m_prev = m_ref[:rows, :]
            m_new = jnp.maximum(m_prev, jnp.max(s, axis=1, keepdims=True))
            p = jnp.exp(s - m_new).astype(_bf16)
            acc_ref[:rows, :] = jnp.exp(m_prev - m_new) * acc_ref[:rows, :] + _dot(p, v)
            m_ref[:rows, :] = m_new

        for si, n_keys in enumerate(segments):
            k_ref, v_ref = kv_refs[2 * si], kv_refs[2 * si + 1]
            kc = slice(k_col * LANES, (k_col + 1) * LANES)
            vc = slice(v_col * LANES, (v_col + 1) * LANES)
            tk = min(TK, n_keys)
            if n_keys == tk:
                step(k_ref[:, kc], v_ref[:, vc])
            else:
                def body(j, carry, k_ref=k_ref, v_ref=v_ref, kc=kc, vc=vc, tk=tk):
                    start = pl.multiple_of(j * tk, tk)
                    step(k_ref[pl.ds(start, tk), kc], v_ref[pl.ds(start, tk), vc])
                    return carry
                lax.fori_loop(0, n_keys // tk, body, 0)

        if mode == "gqa":
            for si, (c, half) in enumerate(stack):
                acc = acc_ref[si * tq:(si + 1) * tq, :]
                sum_lane = HALF if half == 0 else 0
                denom = jnp.sum(jnp.where(lane == sum_lane, acc, 0.0), axis=1, keepdims=True)
                out = (acc / denom).astype(o_ref.dtype)
                lanes = slice(half * HALF, (half + 1) * HALF)
                o_ref[:, c * LANES + half * HALF:c * LANES + (half + 1) * HALF] = out[:, lanes]
        else:
            (c, _), _ = stack
            lv = lv_ref[...]
            lam = (jnp.exp(jnp.sum(lv[0:1] * lv[1:2], axis=1, keepdims=True))
                   - jnp.exp(jnp.sum(lv[2:3] * lv[3:4], axis=1, keepdims=True)) + lambda_init)
            outs = []
            for si in range(2):
                acc = acc_ref[si * tq:(si + 1) * tq, :]
                denom = jnp.sum(jnp.where(lane == C_V_DIM, acc, 0.0), axis=1, keepdims=True)
                outs.append(acc / denom)
            y = jnp.where(lane < C_V_DIM, outs[0] - lam * outs[1], 0.0)
            ms = jnp.sum(y * y, axis=1, keepdims=True) * (1.0 / C_V_DIM)
            y = y * lax.rsqrt(ms + EPS) * gs_ref[...] * (1.0 - lambda_init)
            o_ref[:, c * LANES:(c + 1) * LANES] = y.astype(o_ref.dtype)


def _resident(block, index_map):
    return pl.BlockSpec(block, index_map, pipeline_mode=pl.Buffered(1))


def _attention_a(qa, ka, va, *, batch, n_lat, n_ctx, ctx_queries):
    lat_blocks = batch * n_lat // n_ctx
    groups = tuple((tuple((c, j) for c in range(A_GROUP)), 0, j) for j in range(A_KV_HEADS))
    if ctx_queries:
        tq, n_q = n_ctx, 1
        q_map = lambda b, i: (lat_blocks + b, 0)
        segments = (n_ctx,)
        kv_specs = [_resident((n_ctx, KA_W), lambda b, i: (lat_blocks + b, 0)),
                    _resident((n_ctx, VA_W), lambda b, i: (lat_blocks + b, 0))]
        kv_args = [ka, va]
    else:
        tq, n_q = TQ, n_lat // TQ
        q_map = lambda b, i: (b * n_q + i, 0)
        segments = (n_lat, n_ctx)
        kv_specs = [_resident((n_lat, KA_W), lambda b, i: (b, 0)),
                    _resident((n_lat, VA_W), lambda b, i: (b, 0)),
                    _resident((n_ctx, KA_W), lambda b, i: (lat_blocks + b, 0)),
                    _resident((n_ctx, VA_W), lambda b, i: (lat_blocks + b, 0))]
        kv_args = [ka, va, ka, va]
    rows = A_GROUP * tq
    return pl.pallas_call(
        functools.partial(_attn_kernel, groups=groups, segments=segments, mode="gqa", lambda_init=0.0),
        grid=(batch, n_q),
        in_specs=[pl.BlockSpec((tq, QA_W), q_map)] + kv_specs,
        out_specs=pl.BlockSpec((tq, QA_W), lambda b, i: (b * n_q + i, 0)),
        out_shape=jax.ShapeDtypeStruct((batch * n_q * tq, QA_W), _bf16),
        scratch_shapes=[pltpu.VMEM((rows, 1), _f32), pltpu.VMEM((rows, LANES), _f32)],
        compiler_params=pltpu.CompilerParams(
            dimension_semantics=("arbitrary", "arbitrary"), vmem_limit_bytes=VMEM_LIMIT_BYTES),
        name="attn_gqa_ctx" if ctx_queries else "attn_gqa",
    )(qa, *kv_args)


def _attention_c(qc, kc, vc, lam_vecs, g_sub, *, batch, n_lat, n_ctx, ctx_queries, lambda_init):
    lat_blocks = batch * n_lat // n_ctx
    groups = ((((0, 0), (0, 1)), 0, 0),)
    if ctx_queries:
        tq, n_q = n_ctx, 1
        q_map = lambda b, h, i: (lat_blocks + b, h)
        segments = (n_ctx,)
        kv_specs = [_resident((n_ctx, LANES), lambda b, h, i: (lat_blocks + b, h)),
                    _resident((n_ctx, LANES), lambda b, h, i: (lat_blocks + b, h))]
        kv_args = [kc, vc]
    else:
        tq, n_q = TQ, n_lat // TQ
        q_map = lambda b, h, i: (b * n_q + i, h)
        segments = (n_lat, n_ctx)
        kv_specs = [_resident((n_lat, LANES), lambda b, h, i: (b, h)),
                    _resident((n_lat, LANES), lambda b, h, i: (b, h)),
                    _resident((n_ctx, LANES), lambda b, h, i: (lat_blocks + b, h)),
                    _resident((n_ctx, LANES), lambda b, h, i: (lat_blocks + b, h))]
        kv_args = [kc, vc, kc, vc]
    rows = 2 * tq
    return pl.pallas_call(
        functools.partial(_attn_kernel, groups=groups, segments=segments, mode="diff", lambda_init=lambda_init),
        grid=(batch, C_HEADS, n_q),
        in_specs=[pl.BlockSpec((tq, LANES), q_map)] + kv_specs + [
            pl.BlockSpec((4, C_QK_DIM), lambda b, h, i: (0, 0)),
            pl.BlockSpec((1, LANES), lambda b, h, i: (0, 0))],
        out_specs=pl.BlockSpec((tq, LANES), lambda b, h, i: (b * n_q + i, h)),
        out_shape=jax.ShapeDtypeStruct((batch * n_q * tq, C_HEADS * LANES), _bf16),
        scratch_shapes=[pltpu.VMEM((rows, 1), _f32), pltpu.VMEM((rows, LANES), _f32)],
        compiler_params=pltpu.CompilerParams(
            dimension_semantics=("arbitrary", "arbitrary", "arbitrary"), vmem_limit_bytes=VMEM_LIMIT_BYTES),
        name="attn_diff_ctx" if ctx_queries else "attn_diff",
    )(qc, *kv_args, lam_vecs, g_sub)


def _mix_in_columns():
    a_q = A_HEADS * A_HEAD_DIM
    a_kv = A_KV_HEADS * A_HEAD_DIM
    c_qk = 2 * C_HEADS * C_QK_DIM
    src_qa, src_ka, src_va, src_u, src_v, src_qc, src_kc, src_vc = np.cumsum(
        [0, a_q, a_kv, a_kv, B_WIDTH, B_WIDTH, c_qk, c_qk])
    cols = np.full((IN_W,), -1, np.int64)
    d64 = np.arange(A_HEAD_DIM)
    for c in range(A_GROUP):
        for j in range(A_KV_HEADS):
            head = j * A_GROUP + c
            cols[QA_O + c * LANES + j * HALF + d64] = src_qa + head * A_HEAD_DIM + d64
    cols[KA_O + np.arange(a_kv)] = src_ka + np.arange(a_kv)
    cols[VA_O + d64] = src_va + d64
    cols[VA_O + LANES + HALF + d64] = src_va + A_HEAD_DIM + d64
    cols[U_O + np.arange(B_WIDTH)] = src_u + np.arange(B_WIDTH)
    cols[V_O + np.arange(B_WIDTH)] = src_v + np.arange(B_WIDTH)
    d48 = np.arange(C_QK_DIM)
    pad48 = (d48 // 12) * 16 + d48 % 12
    for h in range(C_HEADS):
        for comp in range(2):
            src = (comp * C_HEADS + h) * C_QK_DIM + d48
            cols[QC_O + h * LANES + comp * HALF + pad48] = src_qc + src
            cols[KC_O + h * LANES + comp * HALF + pad48] = src_kc + src
        cols[VC_O + h * LANES + np.arange(C_V_DIM)] = src_vc + h * C_V_DIM + np.arange(C_V_DIM)
    return cols


def _mix_out_rows():
    rows = np.full((OUT_ROWS,), -1, np.int64)
    d64 = np.arange(A_HEAD_DIM)
    for c in range(A_GROUP):
        for j in range(A_KV_HEADS):
            rows[c * LANES + j * HALF + d64] = (j * A_GROUP + c) * A_HEAD_DIM + d64
    rows[QA_W + np.arange(B_WIDTH)] = QA_W + np.arange(B_WIDTH)
    for h in range(C_HEADS):
        rows[QA_W + B_WIDTH + h * LANES + np.arange(C_V_DIM)] = QA_W + B_WIDTH + h * C_V_DIM + np.arange(C_V_DIM)
    return rows


def _take_padded(w, idx, axis):
    valid = jnp.asarray(idx >= 0)
    taken = jnp.take(w, jnp.asarray(np.maximum(idx, 0)), axis=axis)
    shape = [1] * w.ndim
    shape[axis] = -1
    return jnp.where(valid.reshape(shape), taken, 0.0)


def _rope_tables(n_lat):
    t = np.arange(n_lat)
    row = jnp.asarray((t // GRID_W).astype(np.float32))
    col = jnp.asarray((t % GRID_W).astype(np.float32))

    def angles(dim):
        half = dim // 2
        freqs = 1.0 / (ROPE_BASE ** (jnp.arange(0, half, 2, dtype=_f32) / half))
        return row[:, None] * freqs, col[:, None] * freqs

    def slot(dim, pad):
        ar, ac = angles(dim)
        width = ar.shape[1] + pad
        padc = lambda a: jnp.pad(a, ((0, 0), (0, pad)), constant_values=1.0)
        pads = lambda a: jnp.pad(a, ((0, 0), (0, pad)), constant_values=0.0)
        cos = jnp.concatenate([padc(jnp.cos(ar))] * 2 + [padc(jnp.cos(ac))] * 2, axis=1)
        sin = jnp.concatenate([-pads(jnp.sin(ar)), pads(jnp.sin(ar)), -pads(jnp.sin(ac)), pads(jnp.sin(ac))], axis=1)
        assert cos.shape[1] == 4 * width == HALF
        return jnp.tile(cos, (1, 2)), jnp.tile(sin, (1, 2))

    cos_a, sin_a = slot(A_HEAD_DIM, 0)
    cos_c, sin_c = slot(C_QK_DIM, 4)
    tabs = jnp.stack([cos_a, sin_a, cos_c, sin_c])
    ident = jnp.stack([jnp.ones((TM, LANES), _f32), jnp.zeros((TM, LANES), _f32)] * 2)
    return jnp.concatenate([tabs, ident], axis=1)


def kernel(x, c, ctx, c_ctx, w_mod, b_mod, g_ffn1, w_ffn1_in, w_ffn1_out, g_mix, w_mix_in, w_mix_out,
           g_qnorm, g_knorm, g_vnorm, w_spatial, b_spatial, lambda_vecs, g_subln, g_ffn2, w_ffn2_in,
           w_ffn2_out, g_final):
    batch, n_lat, d = x.shape
    n_ctx = ctx.shape[1]
    depth = w_mod.shape[0]
    assert d == D_MODEL and batch < MOD_ROWS
    assert n_lat % TM == 0 and (batch * n_ctx) % TM == 0 and n_lat % TK == 0 and n_lat % n_ctx == 0
    assert n_ctx % CHUNK == 0 and n_lat % GRID_W == 0 and n_lat % TQ == 0
    n_lat_rows = batch * n_lat
    n_rows = n_lat_rows + batch * n_ctx
    n_tiles = n_rows // TM
    n_latent_tiles = n_lat_rows // TM
    tiles_per_batch = n_lat // TM

    cc = jnp.zeros((MOD_ROWS, D_MODEL), _f32).at[:batch].set(c).at[batch].set(c_ctx)
    mod = _modulation(cc, w_mod, b_mod)
    w1i, w1o = w_ffn1_in.astype(_bf16), w_ffn1_out.astype(_bf16)
    w2i, w2o = w_ffn2_in.astype(_bf16), w_ffn2_out.astype(_bf16)
    wmi = _take_padded(w_mix_in, _mix_in_columns(), 2).astype(_bf16)
    wmo = _take_padded(w_mix_out, _mix_out_rows(), 1).astype(_bf16)
    gq = jnp.tile(g_qnorm, (1, 2)).reshape(depth, 1, LANES)
    gk = jnp.tile(g_knorm, (1, 2)).reshape(depth, 1, LANES)
    gsub = jnp.pad(g_subln, ((0, 0), (0, LANES - C_V_DIM))).reshape(depth, 1, LANES)
    ws = w_spatial.astype(_bf16)
    bs = jnp.broadcast_to(jnp.swapaxes(b_spatial, 1, 2)[:, :, :, None],
                          (depth, CHUNK, B_GROUPS, B_GROUP_DIM)).reshape(depth, CHUNK, B_WIDTH)
    tabs = _rope_tables(n_lat)
    tok = jnp.concatenate([x.reshape(n_lat_rows, d), ctx.reshape(batch * n_ctx, d)], axis=0)

    row_vec = lambda g: g.reshape(depth, 1, -1)
    g1, gm, g2, gv = row_vec(g_ffn1), row_vec(g_mix), row_vec(g_ffn2), row_vec(g_vnorm)
    gf = g_final.reshape(1, D_MODEL)

    tile = lambda w: pl.BlockSpec((TM, w), lambda i: (i, 0))
    params = pltpu.CompilerParams(dimension_semantics=("arbitrary",), vmem_limit_bytes=VMEM_LIMIT_BYTES)
    bf = lambda w: jax.ShapeDtypeStruct((n_rows, w), _bf16)
    static = dict(tiles_per_batch=tiles_per_batch, n_latent_tiles=n_latent_tiles, ctx_row=batch)

    for l in range(depth):
        last = l == depth - 1
        layer = lambda shape, l=l: _resident((None,) + shape, lambda i: (l,) + (0,) * len(shape))
        tab_map = lambda i: (0, jnp.where(i < n_latent_tiles, i % tiles_per_batch, tiles_per_batch), 0)
        tok, qa, ka, va, qc, kc, vc, yb = pl.pallas_call(
            functools.partial(_pre_kernel, **static),
            grid=(n_tiles,),
            in_specs=[tile(D_MODEL), layer((N_MOD, MOD_ROWS, D_MODEL)), layer((1, D_MODEL)),
                      layer((D_MODEL, 2 * D_FF)), layer((D_FF, D_MODEL)), layer((1, D_MODEL)),
                      layer((D_MODEL, IN_W)), layer((1, LANES)), layer((1, LANES)), layer((1, B_WIDTH)),
                      layer((B_GROUPS, CHUNK, CHUNK)), layer((CHUNK, B_WIDTH)),
                      pl.BlockSpec((4, TM, LANES), tab_map)],
            out_specs=[tile(D_MODEL), tile(QA_W), tile(KA_W), tile(VA_W), tile(QC_W), tile(KC_W), tile(VC_W),
                       tile(B_WIDTH)],
            out_shape=[jax.ShapeDtypeStruct((n_rows, D_MODEL), _f32), bf(QA_W), bf(KA_W), bf(VA_W), bf(QC_W),
                       bf(KC_W), bf(VC_W), bf(B_WIDTH)],
            scratch_shapes=[pltpu.VMEM((TM, D_FF), _bf16)],
            compiler_params=params,
            name="ffn1_mix_in",
        )(tok, mod, g1, w1i, w1o, gm, wmi, gq, gk, gv, ws, bs, tabs)

        lambda_init = 0.8 - 0.6 * math.exp(-0.3 * l)
        dims = dict(batch=batch, n_lat=n_lat, n_ctx=n_ctx)
        ya = _attention_a(qa, ka, va, ctx_queries=False, **dims)
        yc = _attention_c(qc, kc, vc, lambda_vecs[l], gsub[l], ctx_queries=False, lambda_init=lambda_init, **dims)
        if not last:
            ya = jnp.concatenate([ya, _attention_a(qa, ka, va, ctx_queries=True, **dims)], axis=0)
            yc = jnp.concatenate(
                [yc, _attention_c(qc, kc, vc, lambda_vecs[l], gsub[l], ctx_queries=True, lambda_init=lambda_init,
                                  **dims)], axis=0)
        post_tiles = n_latent_tiles if last else n_tiles
        tok = pl.pallas_call(
            functools.partial(_post_kernel, final_norm=last, **static),
            grid=(post_tiles,),
            in_specs=[tile(D_MODEL), layer((N_MOD, MOD_ROWS, D_MODEL)), tile(QA_W), tile(B_WIDTH),
                      tile(C_HEADS * LANES), layer((OUT_ROWS, D_MODEL)), layer((1, D_MODEL)),
                      layer((D_MODEL, 2 * D_FF)), layer((D_FF, D_MODEL)),
                      pl.BlockSpec((1, D_MODEL), lambda i: (0, 0))],
            out_specs=tile(D_MODEL),
            out_shape=jax.ShapeDtypeStruct((post_tiles * TM, D_MODEL), _f32),
            scratch_shapes=[pltpu.VMEM((TM, D_FF), _bf16)],
            compiler_params=params,
            name="mix_out_ffn2",
        )(tok, mod, ya, yb, yc, wmo, g2, w2i, w2o, gf)

    return tok.reshape(batch, n_lat, d)
```

```python
import functools
import math

import numpy as np
import jax
import jax.numpy as jnp
from jax import lax
from jax.experimental import pallas as pl
from jax.experimental.pallas import tpu as pltpu

D_MODEL = 1024
D_FF = 2816
N_MOD = 9
GRID_W = 64
CHUNK = 128
ROPE_BASE = 10000.0
EPS = 1e-6
LOG2E = math.log2(math.e)
A_HEADS, A_KV_HEADS, A_HEAD_DIM = 6, 2, 64
A_GROUP = A_HEADS // A_KV_HEADS
B_GROUPS, B_GROUP_DIM = 4, 64
B_WIDTH = B_GROUPS * B_GROUP_DIM
C_HEADS, C_QK_DIM = 4, 48
C_V_DIM = 2 * C_QK_DIM

LANES = 128
HALF = LANES // 2
ROPE_BLOCK = 16
VMEM_LIMIT_BYTES = 56 * 1024 * 1024

TM = 512
TQ = 512
TK = 512
FF_CHUNK = 256
MOD_ROWS = 8

KA_W, U_W, V_W, KC_W = 128, 256, 256, 512
KA_O, U_O, V_O, KC_O, NAT_W = [int(v) for v in np.cumsum([0, KA_W, U_W, V_W, KC_W])]
QA_W, VA_W, QC_W, VC_W = 384, 256, 512, 512
QA_R, VA_R, QC_R, VC_R, TR_W = [int(v) for v in np.cumsum([0, QA_W, VA_W, QC_W, VC_W])]
OUT_ROWS = QA_W + B_WIDTH + C_HEADS * LANES

_f32 = jnp.float32
_bf16 = jnp.bfloat16
_NT = (((1,), (1,)), ((), ()))


def _dot(a, b):
    return jnp.dot(a, b, preferred_element_type=_f32)


def _rms(x, gain):
    return x * lax.rsqrt(jnp.mean(x * x, axis=-1, keepdims=True) + EPS) * gain


def _silu(x):
    return x * (1.0 / (1.0 + jnp.exp(-x)))


def _gelu_tanh(x):
    return 0.5 * x * (1.0 + jnp.tanh(math.sqrt(2.0 / math.pi) * (x + 0.044715 * (x * x * x))))


def _lane(shape):
    return lax.broadcasted_iota(jnp.int32, shape, len(shape) - 1)


def _row(shape):
    return lax.broadcasted_iota(jnp.int32, shape, 0)


def _mod_kernel(c_ref, w_ref, b_ref, o_ref):
    sc = _silu(c_ref[...]).astype(_bf16)
    o_ref[...] = _dot(sc, w_ref[...].astype(_bf16)) + b_ref[...]


def _modulation(cc, w_mod, b_mod):
    depth = w_mod.shape[0]
    return pl.pallas_call(
        _mod_kernel,
        grid=(depth, N_MOD),
        in_specs=[
            pl.BlockSpec((MOD_ROWS, D_MODEL), lambda l, j: (0, 0)),
            pl.BlockSpec((None, D_MODEL, D_MODEL), lambda l, j: (l, 0, j)),
            pl.BlockSpec((None, None, 1, D_MODEL), lambda l, j: (l, j, 0, 0)),
        ],
        out_specs=pl.BlockSpec((None, None, MOD_ROWS, D_MODEL), lambda l, j: (l, j, 0, 0)),
        out_shape=jax.ShapeDtypeStruct((depth, N_MOD, MOD_ROWS, D_MODEL), _f32),
        compiler_params=pltpu.CompilerParams(
            dimension_semantics=("arbitrary", "arbitrary"), vmem_limit_bytes=VMEM_LIMIT_BYTES),
        name="modulation",
    )(cc, w_mod, b_mod.reshape(depth, N_MOD, 1, D_MODEL))


def _mod_row(mod_ref, k, row):
    return mod_ref[k, pl.ds(row, 1), :]


def _swiglu_residual(x, gain, shift, scale, gate, w_in_ref, w_out_ref, act_ref):
    h = (_rms(x, gain) * (1.0 + scale) + shift).astype(_bf16)
    for c in range(D_FF // FF_CHUNK):
        lo = c * FF_CHUNK
        g = _dot(h, w_in_ref[:, lo:lo + FF_CHUNK])
        u = _dot(h, w_in_ref[:, D_FF + lo:D_FF + lo + FF_CHUNK])
        act_ref[:, lo:lo + FF_CHUNK] = (_silu(g) * u).astype(_bf16)
    y = _dot(act_ref[...], w_out_ref[...])
    return x + (0.5 * gate) * y


def _head_mean_square(x):
    r = lax.broadcasted_iota(jnp.int32, (LANES, LANES), 0) // HALF
    c = lax.broadcasted_iota(jnp.int32, (LANES, LANES), 1) // HALF
    blockdiag = jnp.where(r == c, 1.0 / HALF, 0.0).astype(_bf16)
    sq = x * x
    hi = sq.astype(_bf16)
    lo = (sq - hi.astype(_f32)).astype(_bf16)
    return _dot(hi, blockdiag) + _dot(lo, blockdiag)


def _rope(x, cos, sin):
    first = (_lane(x.shape) % (2 * ROPE_BLOCK)) < ROPE_BLOCK
    partner = jnp.where(first, pltpu.roll(x, LANES - ROPE_BLOCK, 1), pltpu.roll(x, ROPE_BLOCK, 1))
    return x * cos + partner * sin


def _rope_t(x, cos, sin):
    blocks = [x[r:r + ROPE_BLOCK] for r in range(0, LANES, ROPE_BLOCK)]
    partner = jnp.concatenate([blocks[b ^ 1] for b in range(len(blocks))], axis=0)
    return x * cos + partner * sin


def _head_rms_t(x):
    halves = []
    for h in range(LANES // HALF):
        xh = x[h * HALF:(h + 1) * HALF]
        halves.append(xh * lax.rsqrt(jnp.mean(xh * xh, axis=0, keepdims=True) + EPS))
    return jnp.concatenate(halves, axis=0)


def _pre_kernel(x_ref, mod_ref, g1_ref, w1i_ref, w1o_ref, gm_ref, wn_ref, wt_ref, gq_ref, gk_ref, gv_ref,
                ws_ref, bs_ref, tab_ref, tabt_ref,
                xo_ref, qa_ref, ka_ref, va_ref, qc_ref, kc_ref, vc_ref, yb_ref,
                act_ref, *, tiles_per_batch, n_latent_tiles, ctx_row):
    i = pl.program_id(0)
    row = jnp.where(i < n_latent_tiles, i // tiles_per_batch, ctx_row)
    m = lambda k: _mod_row(mod_ref, k, row)

    x = _swiglu_residual(x_ref[...], g1_ref[...], m(0), m(1), m(2), w1i_ref, w1o_ref, act_ref)
    xo_ref[...] = x

    h = (_rms(x, gm_ref[...]) * (1.0 + m(4)) + m(3)).astype(_bf16)
    proj = lambda off, width: _dot(h, wn_ref[:, off:off + width])
    proj_t = lambda off, width: lax.dot_general(wt_ref[off:off + width, :], h, _NT,
                                                preferred_element_type=_f32)
    row_t = _row((LANES, TM))

    cos_a, sin_a, cos_c, sin_c = tabt_ref[0], tabt_ref[1], tabt_ref[2], tabt_ref[3]
    for c in range(QA_W // LANES):
        q = _head_rms_t(proj_t(QA_R + c * LANES, LANES)) * gq_ref[...]
        qa_ref[c * LANES:(c + 1) * LANES, :] = (_rope_t(q, cos_a, sin_a) * (LOG2E * A_HEAD_DIM ** -0.5)).astype(_bf16)
    for c in range(C_HEADS):
        q = proj_t(QC_R + c * LANES, LANES)
        qc_ref[c * LANES:(c + 1) * LANES, :] = (_rope_t(q, cos_c, sin_c) * (LOG2E * C_QK_DIM ** -0.5)).astype(_bf16)
    va_ref[:LANES, :] = jnp.where(row_t == HALF, 1.0, proj_t(VA_R, LANES)).astype(_bf16)
    va_ref[LANES:, :] = jnp.where(row_t == 0, 1.0, proj_t(VA_R + LANES, LANES)).astype(_bf16)
    for c in range(C_HEADS):
        v = proj_t(VC_R + c * LANES, LANES)
        vc_ref[c * LANES:(c + 1) * LANES, :] = jnp.where(row_t == C_V_DIM, 1.0, v).astype(_bf16)

    cos_a, sin_a, cos_c, sin_c = tab_ref[0], tab_ref[1], tab_ref[2], tab_ref[3]
    k = proj(KA_O, LANES)
    k = k * lax.rsqrt(_head_mean_square(k) + EPS) * gk_ref[...]
    ka_ref[...] = _rope(k, cos_a, sin_a).astype(_bf16)
    for c in range(C_HEADS):
        k = proj(KC_O + c * LANES, LANES)
        kc_ref[:, c * LANES:(c + 1) * LANES] = _rope(k, cos_c, sin_c).astype(_bf16)

    u = _gelu_tanh(proj(U_O, U_W))
    v = _rms(_gelu_tanh(proj(V_O, V_W)), gv_ref[...]).astype(_bf16)
    lo_half = _lane((CHUNK, LANES)) < HALF
    for t in range(TM // CHUNK):
        rows = slice(t * CHUNK, (t + 1) * CHUNK)
        for c in range(B_WIDTH // LANES):
            cols = slice(c * LANES, (c + 1) * LANES)
            vt = v[rows, cols]
            mixed = jnp.where(lo_half, _dot(ws_ref[2 * c], vt), _dot(ws_ref[2 * c + 1], vt)) + bs_ref[:, cols]
            yb_ref[rows, cols] = (u[rows, cols] * mixed).astype(_bf16)


def _post_kernel(x_ref, mod_ref, ya_ref, yb_ref, yc_ref, wo_ref, g2_ref, w2i_ref, w2o_ref, gf_ref,
                 xo_ref, act_ref, *, tiles_per_batch, n_latent_tiles, ctx_row, final_norm):
    i = pl.program_id(0)
    row = jnp.where(i < n_latent_tiles, i // tiles_per_batch, ctx_row)
    m = lambda k: _mod_row(mod_ref, k, row)
    y = (_dot(ya_ref[...], wo_ref[:QA_W, :]) + _dot(yb_ref[...], wo_ref[QA_W:QA_W + B_WIDTH, :])
         + _dot(yc_ref[...], wo_ref[QA_W + B_WIDTH:, :]))
    x = x_ref[...] + m(5) * y
    x = _swiglu_residual(x, g2_ref[...], m(6), m(7), m(8), w2i_ref, w2o_ref, act_ref)
    if final_norm:
        x = _rms(x, gf_ref[...])
    xo_ref[...] = x


def _attn_kernel(*refs, groups, segments, mode, lambda_init):
    n_seg = len(segments)
    q_ref = refs[0]
    kv_refs = refs[1:1 + 2 * n_seg]
    pos = 1 + 2 * n_seg
    if mode == "diff":
        lv_ref, gs_ref = refs[pos], refs[pos + 1]
        pos += 2
    o_ref = refs[pos]
    s_a, s_b, m_ref, acc_ref = refs[pos + 1:pos + 5]

    tq = q_ref.shape[1]
    lo_rows = _row((LANES, tq)) < HALF

    for g, (stack, k_col, v_col) in enumerate(groups):
        r = len(stack) * tq
        qs = jnp.concatenate(
            [jnp.where(lo_rows if half == 0 else jnp.logical_not(lo_rows),
                       q_ref[c * LANES:(c + 1) * LANES, :], jnp.zeros((), _bf16))
             for c, half in stack], axis=1)
        m_ref[:, :r] = jnp.full((1, r), -1e30, _f32)
        acc_ref[g, :, :r] = jnp.zeros((LANES, r), _f32)
        kc = slice(k_col * LANES, (k_col + 1) * LANES)
        vr = slice(v_col * LANES, (v_col + 1) * LANES)

        def scores(k):
            return _dot(k, qs)

        def update(s, vt):
            m_prev = m_ref[:, :r]
            m_new = jnp.maximum(m_prev, jnp.max(s, axis=0, keepdims=True))
            p = jnp.exp2(s - m_new).astype(_bf16)
            acc_ref[g, :, :r] = jnp.exp2(m_prev - m_new) * acc_ref[g, :, :r] + _dot(vt, p)
            m_ref[:, :r] = m_new

        for si, n_keys in enumerate(segments):
            k_ref, v_ref = kv_refs[2 * si], kv_refs[2 * si + 1]
            if n_keys <= TK:
                update(scores(k_ref[:, kc]), v_ref[vr, :])
                continue
            n = n_keys // TK
            assert n * TK == n_keys and n % 2 == 0

            def k_at(j, k_ref=k_ref):
                return k_ref[pl.ds(pl.multiple_of(j * TK, TK), TK), kc]

            def v_at(j, v_ref=v_ref):
                return v_ref[vr, pl.ds(pl.multiple_of(j * TK, TK), TK)]

            s_a[:, :r] = scores(k_at(0))

            def body(i, carry):
                j = 2 * i
                s_b[:, :r] = scores(k_at(j + 1))
                update(s_a[:, :r], v_at(j))
                s_a[:, :r] = scores(k_at(jnp.minimum(j + 2, n - 1)))
                update(s_b[:, :r], v_at(j + 1))
                return carry

            lax.fori_loop(0, n // 2, body, 0)

    if mode == "gqa":
        for si, (c, _) in enumerate(groups[0][0]):
            cols = slice(si * tq, (si + 1) * tq)
            a0, a1 = acc_ref[0, :, cols], acc_ref[1, :, cols]
            out_t = jnp.where(lo_rows, a0 / a0[HALF:HALF + 1], a1 / a1[0:1])
            o_ref[:, c * LANES:(c + 1) * LANES] = out_t.T.astype(o_ref.dtype)
    else:
        lv = lv_ref[...]
        lam = (jnp.exp(jnp.sum(lv[0:1] * lv[1:2], axis=1, keepdims=True))
               - jnp.exp(jnp.sum(lv[2:3] * lv[3:4], axis=1, keepdims=True)) + lambda_init)
        a0, a1 = acc_ref[0, :, :tq], acc_ref[0, :, tq:2 * tq]
        y = a0 / a0[C_V_DIM:C_V_DIM + 1] - lam * (a1 / a1[C_V_DIM:C_V_DIM + 1])
        y = jnp.where(_row((LANES, tq)) < C_V_DIM, y, 0.0)
        ms = jnp.sum(y * y, axis=0, keepdims=True) * (1.0 / C_V_DIM)
        y = y * lax.rsqrt(ms + EPS) * gs_ref[...] * (1.0 - lambda_init)
        o_ref[...] = y.T.astype(o_ref.dtype)


def _resident(block, index_map):
    return pl.BlockSpec(block, index_map, pipeline_mode=pl.Buffered(1))


def _attn_scratch(n_groups, r):
    return [pltpu.VMEM((TK, r), _f32), pltpu.VMEM((TK, r), _f32), pltpu.VMEM((1, r), _f32),
            pltpu.VMEM((n_groups, LANES, r), _f32)]


def _attention_a(qa, ka, va, *, batch, n_lat, n_ctx, ctx_queries):
    lat_blocks = batch * n_lat // n_ctx
    groups = tuple((tuple((c, j) for c in range(A_GROUP)), 0, j) for j in range(A_KV_HEADS))
    ctx_specs = [_resident((n_ctx, KA_W), lambda b, i: (lat_blocks + b, 0)),
                 _resident((VA_W, n_ctx), lambda b, i: (0, lat_blocks + b))]
    if ctx_queries:
        tq, n_q = n_ctx, 1
        q_map = lambda b, i: (0, lat_blocks + b)
        segments, kv_specs, kv_args = (n_ctx,), ctx_specs, [ka, va]
    else:
        tq, n_q = TQ, n_lat // TQ
        q_map = lambda b, i: (0, b * n_q + i)
        segments = (n_ctx, n_lat)
        kv_specs = ctx_specs + [_resident((n_lat, KA_W), lambda b, i: (b, 0)),
                                _resident((VA_W, n_lat), lambda b, i: (0, b))]
        kv_args = [ka, va, ka, va]
    return pl.pallas_call(
        functools.partial(_attn_kernel, groups=groups, segments=segments, mode="gqa", lambda_init=0.0),
        grid=(batch, n_q),
        in_specs=[pl.BlockSpec((QA_W, tq), q_map)] + kv_specs,
        out_specs=pl.BlockSpec((tq, QA_W), lambda b, i: (b * n_q + i, 0)),
        out_shape=jax.ShapeDtypeStruct((batch * n_q * tq, QA_W), _bf16),
        scratch_shapes=_attn_scratch(A_KV_HEADS, A_GROUP * tq),
        compiler_params=pltpu.CompilerParams(
            dimension_semantics=("arbitrary", "arbitrary"), vmem_limit_bytes=VMEM_LIMIT_BYTES),
        name="attn_gqa_ctx" if ctx_queries else "attn_gqa",
    )(qa, *kv_args)


def _attention_c(qc, kc, vc, lam_vecs, g_sub, *, batch, n_lat, n_ctx, ctx_queries, lambda_init):
    lat_blocks = batch * n_lat // n_ctx
    groups = ((((0, 0), (0, 1)), 0, 0),)
    ctx_specs = [_resident((n_ctx, LANES), lambda b, h, i: (lat_blocks + b, h)),
                 _resident((LANES, n_ctx), lambda b, h, i: (h, lat_blocks + b))]
    if ctx_queries:
        tq, n_q = n_ctx, 1
        q_map = lambda b, h, i: (h, lat_blocks + b)
        segments, kv_specs, kv_args = (n_ctx,), ctx_specs, [kc, vc]
    else:
        tq, n_q = TQ, n_lat // TQ
        q_map = lambda b, h, i: (h, b * n_q + i)
        segments = (n_ctx, n_lat)
        kv_specs = ctx_specs + [_resident((n_lat, LANES), lambda b, h, i: (b, h)),
                                _resident((LANES, n_lat), lambda b, h, i: (h, b))]
        kv_args = [kc, vc, kc, vc]
    return pl.pallas_call(
        functools.partial(_attn_kernel, groups=groups, segments=segments, mode="diff", lambda_init=lambda_init),
        grid=(batch, C_HEADS, n_q),
        in_specs=[pl.BlockSpec((LANES, tq), q_map)] + kv_specs + [
            pl.BlockSpec((4, C_QK_DIM), lambda b, h, i: (0, 0)),
            pl.BlockSpec((LANES, 1), lambda b, h, i: (0, 0))],
        out_specs=pl.BlockSpec((tq, LANES), lambda b, h, i: (b * n_q + i, h)),
        out_shape=jax.ShapeDtypeStruct((batch * n_q * tq, C_HEADS * LANES), _bf16),
        scratch_shapes=_attn_scratch(1, 2 * tq),
        compiler_params=pltpu.CompilerParams(
            dimension_semantics=("arbitrary", "arbitrary", "arbitrary"), vmem_limit_bytes=VMEM_LIMIT_BYTES),
        name="attn_diff_ctx" if ctx_queries else "attn_diff",
    )(qc, *kv_args, lam_vecs, g_sub)


def _mix_in_sources():
    a_q = A_HEADS * A_HEAD_DIM
    a_kv = A_KV_HEADS * A_HEAD_DIM
    c_qk = 2 * C_HEADS * C_QK_DIM
    src_qa, src_ka, src_va, src_u, src_v, src_qc, src_kc, src_vc = np.cumsum(
        [0, a_q, a_kv, a_kv, B_WIDTH, B_WIDTH, c_qk, c_qk])
    nat = np.full((NAT_W,), -1, np.int64)
    tr = np.full((TR_W,), -1, np.int64)
    d64 = np.arange(A_HEAD_DIM)
    for c in range(A_GROUP):
        for j in range(A_KV_HEADS):
            head = j * A_GROUP + c
            tr[QA_R + c * LANES + j * HALF + d64] = src_qa + head * A_HEAD_DIM + d64
    nat[KA_O + np.arange(a_kv)] = src_ka + np.arange(a_kv)
    tr[VA_R + d64] = src_va + d64
    tr[VA_R + LANES + HALF + d64] = src_va + A_HEAD_DIM + d64
    nat[U_O + np.arange(B_WIDTH)] = src_u + np.arange(B_WIDTH)
    nat[V_O + np.arange(B_WIDTH)] = src_v + np.arange(B_WIDTH)
    d48 = np.arange(C_QK_DIM)
    rot = C_QK_DIM // 4
    pad48 = (d48 // rot) * ROPE_BLOCK + d48 % rot
    for h in range(C_HEADS):
        for comp in range(2):
            src = (comp * C_HEADS + h) * C_QK_DIM + d48
            tr[QC_R + h * LANES + comp * HALF + pad48] = src_qc + src
            nat[KC_O + h * LANES + comp * HALF + pad48] = src_kc + src
        tr[VC_R + h * LANES + np.arange(C_V_DIM)] = src_vc + h * C_V_DIM + np.arange(C_V_DIM)
    return nat, tr


def _mix_out_rows():
    rows = np.full((OUT_ROWS,), -1, np.int64)
    d64 = np.arange(A_HEAD_DIM)
    for c in range(A_GROUP):
        for j in range(A_KV_HEADS):
            rows[c * LANES + j * HALF + d64] = (j * A_GROUP + c) * A_HEAD_DIM + d64
    rows[QA_W + np.arange(B_WIDTH)] = QA_W + np.arange(B_WIDTH)
    for h in range(C_HEADS):
        rows[QA_W + B_WIDTH + h * LANES + np.arange(C_V_DIM)] = QA_W + B_WIDTH + h * C_V_DIM + np.arange(C_V_DIM)
    return rows


def _take_padded(w, idx, axis):
    valid = jnp.asarray(idx >= 0)
    taken = jnp.take(w, jnp.asarray(np.maximum(idx, 0)), axis=axis)
    shape = [1] * w.ndim
    shape[axis] = -1
    return jnp.where(valid.reshape(shape), taken, 0.0)


def _rope_tables(n_lat):
    t = np.arange(n_lat)
    row = jnp.asarray((t // GRID_W).astype(np.float32))
    col = jnp.asarray((t % GRID_W).astype(np.float32))

    def angles(dim):
        half = dim // 2
        freqs = 1.0 / (ROPE_BASE ** (jnp.arange(0, half, 2, dtype=_f32) / half))
        return row[:, None] * freqs, col[:, None] * freqs

    def slot(dim):
        ar, ac = angles(dim)
        pad = ROPE_BLOCK - ar.shape[1]
        padc = lambda a: jnp.pad(a, ((0, 0), (0, pad)), constant_values=1.0)
        pads = lambda a: jnp.pad(a, ((0, 0), (0, pad)), constant_values=0.0)
        cos = jnp.concatenate([padc(jnp.cos(ar))] * 2 + [padc(jnp.cos(ac))] * 2, axis=1)
        sin = jnp.concatenate([-pads(jnp.sin(ar)), pads(jnp.sin(ar)), -pads(jnp.sin(ac)), pads(jnp.sin(ac))], axis=1)
        return jnp.tile(cos, (1, 2)), jnp.tile(sin, (1, 2))

    cos_a, sin_a = slot(A_HEAD_DIM)
    cos_c, sin_c = slot(C_QK_DIM)
    tabs = jnp.stack([cos_a, sin_a, cos_c, sin_c])
    ident = jnp.stack([jnp.ones((TM, LANES), _f32), jnp.zeros((TM, LANES), _f32)] * 2)
    return jnp.concatenate([tabs, ident], axis=1)


def kernel(x, c, ctx, c_ctx, w_mod, b_mod, g_ffn1, w_ffn1_in, w_ffn1_out, g_mix, w_mix_in, w_mix_out,
           g_qnorm, g_knorm, g_vnorm, w_spatial, b_spatial, lambda_vecs, g_subln, g_ffn2, w_ffn2_in,
           w_ffn2_out, g_final):
    batch, n_lat, d = x.shape
    n_ctx = ctx.shape[1]
    depth = w_mod.shape[0]
    assert d == D_MODEL and batch < MOD_ROWS
    assert n_lat % TM == 0 and (batch * n_ctx) % TM == 0 and n_lat % (2 * TK) == 0 and n_lat % n_ctx == 0
    assert n_ctx % CHUNK == 0 and n_ctx <= TK and n_lat % GRID_W == 0 and n_lat % TQ == 0 and n_ctx % LANES == 0
    n_lat_rows = batch * n_lat
    n_rows = n_lat_rows + batch * n_ctx
    n_tiles = n_rows // TM
    n_latent_tiles = n_lat_rows // TM
    tiles_per_batch = n_lat // TM

    cc = jnp.zeros((MOD_ROWS, D_MODEL), _f32).at[:batch].set(c).at[batch].set(c_ctx)
    mod = _modulation(cc, w_mod, b_mod)
    w1i, w1o = w_ffn1_in.astype(_bf16), w_ffn1_out.astype(_bf16)
    w2i, w2o = w_ffn2_in.astype(_bf16), w_ffn2_out.astype(_bf16)
    nat_src, tr_src = _mix_in_sources()
    wn = _take_padded(w_mix_in, nat_src, 2).astype(_bf16)
    wt = jnp.swapaxes(_take_padded(w_mix_in, tr_src, 2), 1, 2).astype(_bf16)
    wmo = _take_padded(w_mix_out, _mix_out_rows(), 1).astype(_bf16)
    gq = jnp.tile(g_qnorm, (1, 2)).reshape(depth, LANES, 1)
    gk = jnp.tile(g_knorm, (1, 2)).reshape(depth, 1, LANES)
    gsub = jnp.pad(g_subln, ((0, 0), (0, LANES - C_V_DIM))).reshape(depth, LANES, 1)
    ws = w_spatial.astype(_bf16)
    bs = jnp.broadcast_to(jnp.swapaxes(b_spatial, 1, 2)[:, :, :, None],
                          (depth, CHUNK, B_GROUPS, B_GROUP_DIM)).reshape(depth, CHUNK, B_WIDTH)
    tabs = _rope_tables(n_lat)
    tabs_t = jnp.swapaxes(tabs, 1, 2)
    tok = jnp.concatenate([x.reshape(n_lat_rows, d), ctx.reshape(batch * n_ctx, d)], axis=0)

    row_vec = lambda g: g.reshape(depth, 1, -1)
    g1, gm, g2, gv = row_vec(g_ffn1), row_vec(g_mix), row_vec(g_ffn2), row_vec(g_vnorm)
    gf = g_final.reshape(1, D_MODEL)

    tile = lambda w: pl.BlockSpec((TM, w), lambda i: (i, 0))
    tile_t = lambda w: pl.BlockSpec((w, TM), lambda i: (0, i))
    params = pltpu.CompilerParams(dimension_semantics=("arbitrary",), vmem_limit_bytes=VMEM_LIMIT_BYTES)
    bf = lambda w: jax.ShapeDtypeStruct((n_rows, w), _bf16)
    bf_t = lambda w: jax.ShapeDtypeStruct((w, n_rows), _bf16)
    static = dict(tiles_per_batch=tiles_per_batch, n_latent_tiles=n_latent_tiles, ctx_row=batch)
    tab_tile = lambda i: jnp.where(i < n_latent_tiles, i % tiles_per_batch, tiles_per_batch)

    for l in range(depth):
        last = l == depth - 1
        layer = lambda shape, l=l: _resident((None,) + shape, lambda i: (l,) + (0,) * len(shape))
        tok, qa, ka, va, qc, kc, vc, yb = pl.pallas_call(
            functools.partial(_pre_kernel, **static),
            grid=(n_tiles,),
            in_specs=[tile(D_MODEL), layer((N_MOD, MOD_ROWS, D_MODEL)), layer((1, D_MODEL)),
                      layer((D_MODEL, 2 * D_FF)), layer((D_FF, D_MODEL)), layer((1, D_MODEL)),
                      layer((D_MODEL, NAT_W)), layer((TR_W, D_MODEL)),
                      layer((LANES, 1)), layer((1, LANES)), layer((1, B_WIDTH)),
                      layer((B_GROUPS, CHUNK, CHUNK)), layer((CHUNK, B_WIDTH)),
                      pl.BlockSpec((4, TM, LANES), lambda i: (0, tab_tile(i), 0)),
                      pl.BlockSpec((4, LANES, TM), lambda i: (0, 0, tab_tile(i)))],
            out_specs=[tile(D_MODEL), tile_t(QA_W), tile(KA_W), tile_t(VA_W), tile_t(QC_W), tile(KC_W),
                       tile_t(VC_W), tile(B_WIDTH)],
            out_shape=[jax.ShapeDtypeStruct((n_rows, D_MODEL), _f32), bf_t(QA_W), bf(KA_W), bf_t(VA_W),
                       bf_t(QC_W), bf(KC_W), bf_t(VC_W), bf(B_WIDTH)],
            scratch_shapes=[pltpu.VMEM((TM, D_FF), _bf16)],
            compiler_params=params,
            name="ffn1_mix_in",
        )(tok, mod, g1, w1i, w1o, gm, wn, wt, gq, gk, gv, ws, bs, tabs, tabs_t)

        lambda_init = 0.8 - 0.6 * math.exp(-0.3 * l)
        dims = dict(batch=batch, n_lat=n_lat, n_ctx=n_ctx)
        ya = _attention_a(qa, ka, va, ctx_queries=False, **dims)
        yc = _attention_c(qc, kc, vc, lambda_vecs[l], gsub[l], ctx_queries=False, lambda_init=lambda_init, **dims)
        if not last:
            ya = jnp.concatenate([ya, _attention_a(qa, ka, va, ctx_queries=True, **dims)], axis=0)
            yc = jnp.concatenate(
                [yc, _attention_c(qc, kc, vc, lambda_vecs[l], gsub[l], ctx_queries=True, lambda_init=lambda_init,
                                  **dims)], axis=0)
        post_tiles = n_latent_tiles if last else n_tiles
        tok = pl.pallas_call(
            functools.partial(_post_kernel, final_norm=last, **static),
            grid=(post_tiles,),
            in_specs=[tile(D_MODEL), layer((N_MOD, MOD_ROWS, D_MODEL)), tile(QA_W), tile(B_WIDTH),
                      tile(C_HEADS * LANES), layer((OUT_ROWS, D_MODEL)), layer((1, D_MODEL)),
                      layer((D_MODEL, 2 * D_FF)), layer((D_FF, D_MODEL)),
                      pl.BlockSpec((1, D_MODEL), lambda i: (0, 0))],
            out_specs=tile(D_MODEL),
            out_shape=jax.ShapeDtypeStruct((post_tiles * TM, D_MODEL), _f32),
            scratch_shapes=[pltpu.VMEM((TM, D_FF), _bf16)],
            compiler_params=params,
            name="mix_out_ffn2",
        )(tok, mod, ya, yb, yc, wmo, g2, w2i, w2o, gf)

    return tok.reshape(batch, n_lat, d)
```
